```python
import math
import jax, jax.numpy as jnp
from jax import lax
import numpy as np

D_MODEL = 1024
BATCH = 8
SEQ = 2048
DEPTH = 1

SC_WIDTH = D_MODEL
SC_GROUPS = 16
SC_KERNEL = 3
SSM_EXPAND = 2
SSM_INNER = SSM_EXPAND * D_MODEL
SSM_HEADDIM = 64
SSM_HEADS = SSM_INNER // SSM_HEADDIM
SSM_GROUPS = 8
SSM_STATE = 128
SSM_CONV = 4
SSM_CHUNK = 128
SSM_CONV_DIM = SSM_INNER + 2 * SSM_GROUPS * SSM_STATE
D_FF = 4 * D_MODEL
EPS = 1e-6

COL_SC = 3 * SC_WIDTH
COL_SSM = SSM_INNER + SSM_CONV_DIM + SSM_HEADS
COL_GATE = 2 * D_MODEL
D_IN_PROJ = COL_SC + COL_SSM + COL_GATE

kernel_name = "hybrid_shortconv_ssd_gated_merge"


def rmsnorm(x, w):
    xf = x.astype(jnp.float32)
    xf = xf * lax.rsqrt(jnp.mean(xf * xf, axis=-1, keepdims=True) + EPS)
    return xf.astype(x.dtype) * w


def causal_depthwise_conv(u, w):
    K = w.shape[0]
    L = u.shape[1]
    up = jnp.pad(u, ((0, 0), (K - 1, 0), (0, 0)))
    y = up[:, 0:L] * w[0]
    for k in range(1, K):
        y = y + up[:, k:k + L] * w[k]
    return y


def ssd_chunked(xh, dt, A, Bg, Cg):
    b, l, h, p = xh.shape
    g, n = Bg.shape[2], Bg.shape[3]
    r = h // g
    q = SSM_CHUNK
    c = l // q
    x = xh.astype(jnp.float32).reshape(b, c, q, g, r, p)
    dt = dt.astype(jnp.float32).reshape(b, c, q, g, r)
    B = Bg.astype(jnp.float32).reshape(b, c, q, g, n)
    C = Cg.astype(jnp.float32).reshape(b, c, q, g, n)
    dA = dt * A.astype(jnp.float32).reshape(g, r)
    dA_cs = jnp.cumsum(dA, axis=2)
    xdt = x * dt[..., None]
    seg = dA_cs[:, :, :, None] - dA_cs[:, :, None, :]
    mask = jnp.tril(jnp.ones((q, q), dtype=bool))[:, :, None, None]
    Lmat = jnp.exp(jnp.where(mask, seg, -jnp.inf))
    CB = jnp.einsum('bcign,bcjgn->bcijg', C, B)
    W = CB[..., None] * Lmat
    y_diag = jnp.einsum('bcijgr,bcjgrp->bcigrp', W, xdt)
    decay = jnp.exp(dA_cs[:, :, -1:] - dA_cs)
    states = jnp.einsum('bcjgn,bcjgr,bcjgrp->bcgrpn', B, decay, xdt)
    chunk_decay = jnp.exp(dA_cs[:, :, -1])

    def step(carry, inp):
        s_c, d_c = inp
        new = carry * d_c[..., None, None] + s_c
        return new, carry

    init = jnp.zeros((b, g, r, p, n), jnp.float32)
    _, prev = lax.scan(step, init, (jnp.moveaxis(states, 1, 0), jnp.moveaxis(chunk_decay, 1, 0)))
    prev = jnp.moveaxis(prev, 0, 1)
    y_off = jnp.einsum('bcign,bcgrpn,bcigr->bcigrp', C, prev, jnp.exp(dA_cs))
    return (y_diag + y_off).reshape(b, l, h, p)


def setup_inputs(seed: int = 0) -> dict:
    key = jax.random.key(seed)
    ks = jax.random.split(key, 20)
    f32 = jnp.float32
    nrm = lambda k, shape, fan: jax.random.normal(k, shape, f32) * (fan ** -0.5)
    dt0 = jnp.exp(jax.random.uniform(ks[9], (SSM_HEADS,), f32, math.log(1e-3), math.log(1e-1)))
    return {
        "x": jax.random.normal(ks[0], (BATCH, SEQ, D_MODEL), f32),
        "norm_mix": 1.0 + 0.02 * jax.random.normal(ks[1], (D_MODEL,), f32),
        "w_in": nrm(ks[2], (D_MODEL, D_IN_PROJ), D_MODEL),
        "b_gate": 0.02 * jax.random.normal(ks[3], (COL_GATE,), f32),
        "sc_conv_w": nrm(ks[4], (SC_KERNEL, SC_WIDTH), SC_KERNEL),
        "ssm_conv_w": nrm(ks[5], (SSM_CONV, SSM_CONV_DIM), SSM_CONV),
        "ssm_conv_b": 0.02 * jax.random.normal(ks[6], (SSM_CONV_DIM,), f32),
        "dt_bias": dt0 + jnp.log(-jnp.expm1(-dt0)),
        "A_log": jnp.log(jax.random.uniform(ks[7], (SSM_HEADS,), f32, 1.0, 16.0)),
        "D_skip": 1.0 + 0.02 * jax.random.normal(ks[8], (SSM_HEADS,), f32),
        "ssm_norm_w": 1.0 + 0.02 * jax.random.normal(ks[10], (SSM_INNER,), f32),
        "w_branch_sc": nrm(ks[11], (SC_WIDTH, D_MODEL), SC_WIDTH),
        "w_branch_ssm": nrm(ks[12], (SSM_INNER, D_MODEL), SSM_INNER),
        "w_out": nrm(ks[13], (D_MODEL, D_MODEL), D_MODEL),
        "norm_mlp": 1.0 + 0.02 * jax.random.normal(ks[14], (D_MODEL,), f32),
        "w_mlp1": nrm(ks[15], (D_MODEL, D_FF), D_MODEL),
        "w_mlp2": nrm(ks[16], (D_FF, D_MODEL), D_FF),
        "norm_final": 1.0 + 0.02 * jax.random.normal(ks[17], (D_MODEL,), f32),
    }


def reference(x, norm_mix, w_in, b_gate, sc_conv_w, ssm_conv_w, ssm_conv_b, dt_bias, A_log,
              D_skip, ssm_norm_w, w_branch_sc, w_branch_ssm, w_out, norm_mlp, w_mlp1, w_mlp2,
              norm_final):
    b, l, _ = x.shape
    for _layer in range(DEPTH):
        h = rmsnorm(x, norm_mix)
        proj = h @ w_in
        sc_part = proj[..., :COL_SC]
        ssm_part = proj[..., COL_SC:COL_SC + COL_SSM]
        gate_part = proj[..., COL_SC + COL_SSM:] + b_gate

        B_sc = sc_part[..., :SC_WIDTH]
        C_sc = sc_part[..., SC_WIDTH:2 * SC_WIDTH]
        x_sc = sc_part[..., 2 * SC_WIDTH:]
        y_a = B_sc * causal_depthwise_conv(C_sc * x_sc, sc_conv_w)
        br_a = y_a @ w_branch_sc

        z = ssm_part[..., :SSM_INNER]
        xBC = ssm_part[..., SSM_INNER:SSM_INNER + SSM_CONV_DIM]
        dt_raw = ssm_part[..., SSM_INNER + SSM_CONV_DIM:]
        xBC = jax.nn.silu(causal_depthwise_conv(xBC, ssm_conv_w) + ssm_conv_b)
        xs = xBC[..., :SSM_INNER].reshape(b, l, SSM_HEADS, SSM_HEADDIM)
        Bg = xBC[..., SSM_INNER:SSM_INNER + SSM_GROUPS * SSM_STATE].reshape(b, l, SSM_GROUPS, SSM_STATE)
        Cg = xBC[..., SSM_INNER + SSM_GROUPS * SSM_STATE:].reshape(b, l, SSM_GROUPS, SSM_STATE)
        dt = jax.nn.softplus(dt_raw.astype(jnp.float32) + dt_bias.astype(jnp.float32))
        A = -jnp.exp(A_log.astype(jnp.float32))
        y = ssd_chunked(xs, dt, A, Bg, Cg) + D_skip.astype(jnp.float32)[:, None] * xs.astype(jnp.float32)
        y = y.reshape(b, l, SSM_INNER)
        yz = (y * jax.nn.silu(z.astype(jnp.float32))).reshape(b, l, SSM_GROUPS, SSM_INNER // SSM_GROUPS)
        yz = yz * lax.rsqrt(jnp.mean(yz * yz, axis=-1, keepdims=True) + EPS)
        y_b = yz.reshape(b, l, SSM_INNER).astype(x.dtype) * ssm_norm_w
        br_b = y_b @ w_branch_ssm

        g = jax.nn.sigmoid(gate_part)
        merged = g[..., :D_MODEL] * br_a + g[..., D_MODEL:] * br_b
        x = x + merged @ w_out

        h2 = rmsnorm(x, norm_mlp)
        x = x + jnp.square(jax.nn.relu(h2 @ w_mlp1)) @ w_mlp2
    return rmsnorm(x, norm_final)
```

```python
import functools
import math

import jax
import jax.numpy as jnp
from jax import lax
from jax.experimental import pallas as pl
from jax.experimental.pallas import tpu as pltpu

F32 = jnp.float32
BF16 = jnp.bfloat16

D_MODEL = 1024
SC_WIDTH = D_MODEL
SC_KERNEL = 3
SSM_INNER = 2048
SSM_HEADDIM = 64
SSM_HEADS = 32
SSM_GROUPS = 8
SSM_STATE = 128
SSM_CONV = 4
SSM_CHUNK = 128
SSM_CONV_DIM = SSM_INNER + 2 * SSM_GROUPS * SSM_STATE
D_FF = 4 * D_MODEL
EPS = 1e-6

HEADS_PER_GROUP = SSM_HEADS // SSM_GROUPS
GROUP_WIDTH = SSM_INNER // SSM_GROUPS
LANES = 128
BF16_SUBLANES = 16

COL_XBC = 0
COL_Z = SSM_CONV_DIM
COL_GATE = COL_Z + SSM_INNER
COL_SC = COL_GATE + 2 * D_MODEL
N_PROJ = COL_SC + 3 * SC_WIDTH

VMEM_LIMIT = 56 * 1024 * 1024


def _sigmoid(v):
    return 1.0 / (1.0 + jnp.exp(-v))


def _rmsnorm(v, w):
    ms = jnp.mean(v * v, axis=-1, keepdims=True)
    return (v * lax.rsqrt(ms + EPS)) * w


IN_TM = 2048
IN_TN = 1024
IN_ROWS = 256


def _inproj_kernel(x_ref, nw_ref, w_ref, wdt_ref, proj_ref, dt_ref, h_ref):
    @pl.when(pl.program_id(1) == 0)
    def _():
        def body(r, carry):
            rows = pl.ds(pl.multiple_of(r * IN_ROWS, IN_ROWS), IN_ROWS)
            hb = _rmsnorm(x_ref[rows, :], nw_ref[...]).astype(BF16)
            h_ref[rows, :] = hb
            dt_ref[rows, :] = jnp.dot(hb, wdt_ref[...], preferred_element_type=F32)
            return carry
        lax.fori_loop(0, IN_TM // IN_ROWS, body, 0)

    proj_ref[...] = jnp.dot(h_ref[...], w_ref[...],
                            preferred_element_type=F32).astype(BF16)


def _in_proj(x2, norm_w, w_main, w_dt):
    t = x2.shape[0]
    return pl.pallas_call(
        _inproj_kernel,
        grid=(t // IN_TM, N_PROJ // IN_TN),
        in_specs=[
            pl.BlockSpec((IN_TM, D_MODEL), lambda i, j: (i, 0)),
            pl.BlockSpec((1, D_MODEL), lambda i, j: (0, 0)),
            pl.BlockSpec((D_MODEL, IN_TN), lambda i, j: (0, j)),
            pl.BlockSpec((D_MODEL, LANES), lambda i, j: (0, 0)),
        ],
        out_specs=[
            pl.BlockSpec((IN_TM, IN_TN), lambda i, j: (i, j)),
            pl.BlockSpec((IN_TM, LANES), lambda i, j: (i, 0)),
        ],
        out_shape=[
            jax.ShapeDtypeStruct((t, N_PROJ), BF16),
            jax.ShapeDtypeStruct((t, LANES), F32),
        ],
        scratch_shapes=[pltpu.VMEM((IN_TM, D_MODEL), BF16)],
        compiler_params=pltpu.CompilerParams(
            dimension_semantics=("arbitrary", "arbitrary"),
            vmem_limit_bytes=VMEM_LIMIT),
        name="in_proj",
    )(x2, norm_w, w_main, w_dt)


Q = SSM_CHUNK
HALO = 8
CONV_COLS = 512


def _ssd_kernel(xbc_ref, z_ref, dt_ref, cw_ref, cb_ref, dtb_ref, a_ref, dexp_ref,
                nw_ref, e_ref, o_ref, xpad_ref, xc_ref, ex_ref, state_ref):
    @pl.when(pl.program_id(1) == 0)
    def _():
        xpad_ref[0:HALO, :] = jnp.zeros((HALO, SSM_CONV_DIM), F32)
        state_ref[...] = jnp.zeros_like(state_ref)

    xpad_ref[HALO:HALO + Q, :] = xbc_ref[...].astype(F32)
    for cb in range(SSM_CONV_DIM // CONV_COLS):
        cols = slice(cb * CONV_COLS, (cb + 1) * CONV_COLS)
        w = cw_ref[:, cols]
        acc = cb_ref[:, cols] + w[0:1, :] * xpad_ref[HALO - 3:HALO - 3 + Q, cols]
        for k in range(1, SSM_CONV):
            acc = acc + w[k:k + 1, :] * xpad_ref[HALO - 3 + k:HALO - 3 + k + Q, cols]
        xc_ref[:, cols] = (acc * _sigmoid(acc)).astype(BF16)
    xpad_ref[0:HALO, :] = xpad_ref[Q:Q + HALO, :]

    lane = lax.broadcasted_iota(jnp.int32, (Q, LANES), 1)
    row = lax.broadcasted_iota(jnp.int32, (Q, LANES), 0)
    head_valid = lane < SSM_HEADS
    causal = row >= lane
    dt_in = dt_ref[...] + dtb_ref[...]
    dt = jnp.maximum(dt_in, 0.0) + jnp.log1p(jnp.exp(-jnp.abs(dt_in)))
    dt = jnp.where(head_valid, dt, 0.0)
    d_a = dt * a_ref[...]
    cs = jnp.dot(causal.astype(F32), d_a, precision=lax.Precision.HIGHEST,
                 preferred_element_type=F32)
    cs_t = cs.T
    dt_t = dt.T
    cs_last = cs[Q - 1:Q, :]
    p_out = jnp.where(head_valid, jnp.exp(cs), 0.0)
    p_state = jnp.where(head_valid, dt * jnp.exp(cs_last - cs), 0.0)

    def split(p):
        hi = p.astype(BF16).astype(F32)
        return (hi + pltpu.roll(p - hi, SSM_HEADS, axis=1)).astype(BF16)

    ex_ref[...] = jnp.dot(jnp.concatenate([split(p_out), split(p_state)], axis=0),
                          e_ref[...], preferred_element_type=F32)

    lane_g = lax.broadcasted_iota(jnp.int32, (Q, GROUP_WIDTH), 1)
    for g in range(SSM_GROUPS):
        cols = slice(g * GROUP_WIDTH, (g + 1) * GROUP_WIDTH)
        bcols = slice(SSM_INNER + g * SSM_STATE, SSM_INNER + (g + 1) * SSM_STATE)
        ccols = slice(SSM_INNER + SSM_GROUPS * SSM_STATE + g * SSM_STATE,
                      SSM_INNER + SSM_GROUPS * SSM_STATE + (g + 1) * SSM_STATE)
        x_g = xc_ref[:, cols]
        b_g = xc_ref[:, bcols]
        c_g = xc_ref[:, ccols]
        cbm = lax.dot_general(c_g, b_g, (((1,), (1,)), ((), ())),
                              preferred_element_type=F32)
        w_parts = []
        x_parts = []
        for hl in range(HEADS_PER_GROUP):
            h = g * HEADS_PER_GROUP + hl
            seg = cs[:, h:h + 1] - cs_t[h:h + 1, :]
            decay = jnp.exp(jnp.where(causal, seg, -jnp.inf)) * dt_t[h:h + 1, :]
            w_parts.append((cbm * decay).astype(BF16))
            in_head = (lane_g >= hl * SSM_HEADDIM) & (lane_g < (hl + 1) * SSM_HEADDIM)
            x_parts.append(jnp.where(in_head, x_g, jnp.zeros_like(x_g)))
        w_cat = jnp.concatenate(w_parts, axis=1)
        x_bd = jnp.concatenate(x_parts, axis=0)
        y = jnp.dot(w_cat, x_bd, preferred_element_type=F32)
        s_prev = state_ref[g]
        e_out = ex_ref[0:Q, cols]
        y = y + e_out * jnp.dot(c_g, s_prev.astype(BF16), preferred_element_type=F32)
        x_f = x_g.astype(F32)
        y = y + dexp_ref[:, cols] * x_f
        xs = (x_f * ex_ref[Q:2 * Q, cols]).astype(BF16)
        state_ref[g] = s_prev * e_out[Q - 1:Q, :] + lax.dot_general(
            b_g, xs, (((0,), (0,)), ((), ())), preferred_element_type=F32)
        z_g = z_ref[:, cols].astype(F32)
        yz = y * (z_g * _sigmoid(z_g))
        ms = jnp.mean(yz * yz, axis=-1, keepdims=True)
        o_ref[:, cols] = ((yz * lax.rsqrt(ms + EPS)) * nw_ref[:, cols]).astype(BF16)


def _ssd(proj, dt_raw, conv_w, conv_b, dt_bias, a_neg, d_exp, norm_w, expand, batch, seq):
    chunks = seq // Q
    t = batch * seq
    full = lambda shape: pl.BlockSpec(shape, lambda b, c: (0,) * len(shape))
    return pl.pallas_call(
        _ssd_kernel,
        grid=(batch, chunks),
        in_specs=[
            pl.BlockSpec((Q, SSM_CONV_DIM), lambda b, c: (b * chunks + c, COL_XBC // SSM_CONV_DIM)),
            pl.BlockSpec((Q, SSM_INNER), lambda b, c: (b * chunks + c, COL_Z // SSM_INNER)),
            pl.BlockSpec((Q, LANES), lambda b, c: (b * chunks + c, 0)),
            full((SSM_CONV, SSM_CONV_DIM)),
            full((1, SSM_CONV_DIM)),
            full((1, LANES)),
            full((1, LANES)),
            full((1, SSM_INNER)),
            full((1, SSM_INNER)),
            full((LANES, SSM_INNER)),
        ],
        out_specs=pl.BlockSpec((Q, SSM_INNER), lambda b, c: (b * chunks + c, 0)),
        out_shape=jax.ShapeDtypeStruct((t, SSM_INNER), BF16),
        scratch_shapes=[
            pltpu.VMEM((HALO + Q, SSM_CONV_DIM), F32),
            pltpu.VMEM((Q, SSM_CONV_DIM), BF16),
            pltpu.VMEM((2 * Q, SSM_INNER), F32),
            pltpu.VMEM((SSM_GROUPS, SSM_STATE, GROUP_WIDTH), F32),
        ],
        compiler_params=pltpu.CompilerParams(
            dimension_semantics=("arbitrary", "arbitrary"),
            vmem_limit_bytes=VMEM_LIMIT),
        name="ssd",
    )(proj, proj, dt_raw, conv_w, conv_b, dt_bias, a_neg, d_exp, norm_w, expand)


POST_TM = 512
SC_HALO = BF16_SUBLANES


def _post_kernel(tiles_per_seq, bsc_ref, csc_ref, xsc_ref, chalo_ref, xhalo_ref, gate_ref,
                 yb_ref, x_ref, scw_ref, bg_ref, wa_ref, wb_ref, wo_ref, o_ref, upad_ref):
    u = csc_ref[...].astype(F32) * xsc_ref[...].astype(F32)
    u_halo = chalo_ref[...].astype(F32) * xhalo_ref[...].astype(F32)
    seq_start = pl.program_id(0) % tiles_per_seq == 0
    upad_ref[0:SC_HALO, :] = jnp.where(seq_start, 0.0, u_halo)
    upad_ref[SC_HALO:SC_HALO + POST_TM, :] = u
    w = scw_ref[...]
    conv = w[SC_KERNEL - 1:SC_KERNEL, :] * u
    for k in range(SC_KERNEL - 1):
        off = SC_HALO - (SC_KERNEL - 1) + k
        conv = conv + w[k:k + 1, :] * upad_ref[off:off + POST_TM, :]
    y_a = (bsc_ref[...].astype(F32) * conv).astype(BF16)
    br_a = jnp.dot(y_a, wa_ref[...], preferred_element_type=F32)
    br_b = jnp.dot(yb_ref[...], wb_ref[...], preferred_element_type=F32)
    gate = _sigmoid(gate_ref[...].astype(F32) + bg_ref[...])
    merged = gate[:, :D_MODEL] * br_a + gate[:, D_MODEL:] * br_b
    o_ref[...] = x_ref[...] + jnp.dot(merged.astype(BF16), wo_ref[...],
                                      preferred_element_type=F32)


def _post(proj, y_b, x2, sc_w, b_gate, w_a, w_b, w_o, seq):
    t = x2.shape[0]
    tiles_per_seq = seq // POST_TM
    halo_blocks = POST_TM // SC_HALO
    sc_blk = COL_SC // SC_WIDTH
    full = lambda shape: pl.BlockSpec(shape, lambda i: (0,) * len(shape))
    halo_idx = lambda col: (lambda i: (jnp.maximum(i * halo_blocks - 1, 0), col))
    return pl.pallas_call(
        functools.partial(_post_kernel, tiles_per_seq),
        grid=(t // POST_TM,),
        in_specs=[
            pl.BlockSpec((POST_TM, SC_WIDTH), lambda i: (i, sc_blk)),
            pl.BlockSpec((POST_TM, SC_WIDTH), lambda i: (i, sc_blk + 1)),
            pl.BlockSpec((POST_TM, SC_WIDTH), lambda i: (i, sc_blk + 2)),
            pl.BlockSpec((SC_HALO, SC_WIDTH), halo_idx(sc_blk + 1)),
            pl.BlockSpec((SC_HALO, SC_WIDTH), halo_idx(sc_blk + 2)),
            pl.BlockSpec((POST_TM, 2 * D_MODEL), lambda i: (i, COL_GATE // (2 * D_MODEL))),
            pl.BlockSpec((POST_TM, SSM_INNER), lambda i: (i, 0)),
            pl.BlockSpec((POST_TM, D_MODEL), lambda i: (i, 0)),
            full((SC_KERNEL, SC_WIDTH)),
            full((1, 2 * D_MODEL)),
            full((SC_WIDTH, D_MODEL)),
            full((SSM_INNER, D_MODEL)),
            full((D_MODEL, D_MODEL)),
        ],
        out_specs=pl.BlockSpec((POST_TM, D_MODEL), lambda i: (i, 0)),
        out_shape=jax.ShapeDtypeStruct((t, D_MODEL), F32),
        scratch_shapes=[pltpu.VMEM((SC_HALO + POST_TM, SC_WIDTH), F32)],
        compiler_params=pltpu.CompilerParams(
            dimension_semantics=("arbitrary",),
            vmem_limit_bytes=VMEM_LIMIT),
        name="post_mixer",
    )(proj, proj, proj, proj, proj, proj, y_b, x2, sc_w, b_gate, w_a, w_b, w_o)


MLP_TM = 512


def _mlp_kernel(x_ref, nw_ref, w1_ref, w2_ref, nf_ref, o_ref):
    x = x_ref[...]
    h = _rmsnorm(x, nw_ref[...]).astype(BF16)
    a = jnp.dot(h, w1_ref[...], preferred_element_type=F32)
    a = jnp.square(jnp.maximum(a, 0.0)).astype(BF16)
    y = x + jnp.dot(a, w2_ref[...], preferred_element_type=F32)
    o_ref[...] = _rmsnorm(y, nf_ref[...])


def _mlp(x1, norm_w, w1, w2, norm_f):
    t = x1.shape[0]
    full = lambda shape: pl.BlockSpec(shape, lambda i: (0,) * len(shape))
    return pl.pallas_call(
        _mlp_kernel,
        grid=(t // MLP_TM,),
        in_specs=[
            pl.BlockSpec((MLP_TM, D_MODEL), lambda i: (i, 0)),
            full((1, D_MODEL)),
            full((D_MODEL, D_FF)),
            full((D_FF, D_MODEL)),
            full((1, D_MODEL)),
        ],
        out_specs=pl.BlockSpec((MLP_TM, D_MODEL), lambda i: (i, 0)),
        out_shape=jax.ShapeDtypeStruct((t, D_MODEL), F32),
        compiler_params=pltpu.CompilerParams(
            dimension_semantics=("arbitrary",),
            vmem_limit_bytes=VMEM_LIMIT),
        name="mlp",
    )(x1, norm_w, w1, w2, norm_f)


def kernel(x, norm_mix, w_in, b_gate, sc_conv_w, ssm_conv_w, ssm_conv_b, dt_bias, A_log,
           D_skip, ssm_norm_w, w_branch_sc, w_branch_ssm, w_out, norm_mlp, w_mlp1, w_mlp2,
           norm_final):
    batch, seq, _ = x.shape
    x2 = x.reshape(batch * seq, D_MODEL)

    c_z = 3 * SC_WIDTH
    c_xbc = c_z + SSM_INNER
    c_dt = c_xbc + SSM_CONV_DIM
    c_gate = c_dt + SSM_HEADS
    w_main = jnp.concatenate(
        [w_in[:, c_xbc:c_dt], w_in[:, c_z:c_xbc], w_in[:, c_gate:], w_in[:, :c_z]],
        axis=1).astype(BF16)
    w_dt = jnp.pad(w_in[:, c_dt:c_gate], ((0, 0), (0, LANES - SSM_HEADS))).astype(BF16)
    pad_heads = lambda v: jnp.pad(v.astype(F32), (0, LANES - SSM_HEADS)).reshape(1, LANES)
    a_neg = pad_heads(-jnp.exp(A_log.astype(F32)))
    d_exp = jnp.repeat(D_skip.astype(F32), SSM_HEADDIM).reshape(1, SSM_INNER)
    r = jnp.arange(LANES)[:, None]
    ch = jnp.arange(SSM_INNER)[None, :]
    expand = ((r < 2 * SSM_HEADS) & (r % SSM_HEADS == ch // SSM_HEADDIM)).astype(BF16)

    proj, dt_raw = _in_proj(x2, norm_mix.reshape(1, D_MODEL), w_main, w_dt)
    y_b = _ssd(proj, dt_raw, ssm_conv_w, ssm_conv_b.reshape(1, SSM_CONV_DIM),
               pad_heads(dt_bias), a_neg, d_exp, ssm_norm_w.reshape(1, SSM_INNER),
               expand, batch, seq)
    x1 = _post(proj, y_b, x2, sc_conv_w, b_gate.reshape(1, 2 * D_MODEL),
               w_branch_sc.astype(BF16), w_branch_ssm.astype(BF16), w_out.astype(BF16), seq)
    out = _mlp(x1, norm_mlp.reshape(1, D_MODEL), w_mlp1.astype(BF16), w_mlp2.astype(BF16),
               norm_final.reshape(1, D_MODEL))
    return out.reshape(batch, seq, D_MODEL)
```
